```python
import math
import jax, jax.numpy as jnp
from jax import lax
import numpy as np

D_MODEL = 4096
BATCH = 4
SEQ = 4096
DEPTH = 1

MIX_WIDTH = D_MODEL
GLA_WIDTH = MIX_WIDTH // 2
S5_WIDTH = MIX_WIDTH - GLA_WIDTH
GLA_HEADS = 4
GLA_DK = GLA_WIDTH // 2 // GLA_HEADS
GLA_DV = GLA_WIDTH // GLA_HEADS
GLA_KEY_WIDTH = GLA_HEADS * GLA_DK
GLA_GATE_RANK = 16
GLA_GATE_NORMALIZER = 16.0
GLA_CHUNK = 64
S5_GROUP = 16
S5_GROUPS = S5_WIDTH // S5_GROUP
S5_STATE = 64
S5_CHUNK = 128
D_FF = ((8 * D_MODEL // 3 + 255) // 256) * 256
N_MOD = 9
EPS = 1e-6
IN_SPLITS = (GLA_KEY_WIDTH,
             2 * GLA_KEY_WIDTH,
             2 * GLA_KEY_WIDTH + GLA_WIDTH,
             2 * GLA_KEY_WIDTH + 2 * GLA_WIDTH,
             2 * GLA_KEY_WIDTH + 2 * GLA_WIDTH + GLA_GATE_RANK)
IN_WIDTH = IN_SPLITS[-1] + S5_WIDTH

kernel_name = "hybrid_gla_s5_macaron_adaln"


def rms_norm(x):
    xf = x.astype(jnp.float32)
    return (xf * lax.rsqrt(jnp.mean(xf * xf, axis=-1, keepdims=True) + EPS)).astype(x.dtype)


def modulate(h, shift, scale):
    return rms_norm(h) * (1.0 + scale[:, None, :]) + shift[:, None, :]


def swiglu(h, w_gate, w_up, w_down):
    return (jax.nn.silu(h @ w_gate) * (h @ w_up)) @ w_down


def gla_chunked(q, k, v, g_log):
    bsz, length, heads, dk = q.shape
    dv = v.shape[-1]
    n_chunks = length // GLA_CHUNK

    def blocks(t):
        return t.reshape(bsz, n_chunks, GLA_CHUNK, heads, t.shape[-1]).transpose(1, 0, 3, 2, 4)

    q, k, v, g = blocks(q), blocks(k), blocks(v), blocks(g_log)
    b = jnp.cumsum(g, axis=-2)
    b_last = b[..., -1:, :]
    q_t = q * jnp.exp(b) * (dk ** -0.5)
    k_t = k * jnp.exp(-b)
    k_dec = k * jnp.exp(b_last - b)
    chunk_decay = jnp.exp(b_last[..., 0, :])
    causal = jnp.tril(jnp.ones((GLA_CHUNK, GLA_CHUNK), dtype=bool))
    att = jnp.where(causal, jnp.einsum('nbhid,nbhjd->nbhij', q_t, k_t), 0.0)
    o_intra = jnp.einsum('nbhij,nbhjv->nbhiv', att, v)

    def step(state, inp):
        q_n, k_n, v_n, d_n = inp
        o_n = jnp.einsum('bhid,bhdv->bhiv', q_n, state)
        state = d_n[..., None] * state + jnp.einsum('bhid,bhiv->bhdv', k_n, v_n)
        return state, o_n

    state0 = jnp.zeros((bsz, heads, dk, dv), jnp.float32)
    _, o_inter = lax.scan(step, state0, (q_t, k_dec, v, chunk_decay))
    o = o_intra + o_inter
    return o.transpose(1, 0, 3, 2, 4).reshape(bsz, length, heads, dv)


def s5_ssm(u, lam_re, lam_im, log_dt, b_re, b_im, c_re, c_im):
    bsz, length, groups, gw = u.shape
    n_chunks = length // S5_CHUNK
    dt = jnp.exp(log_dt.astype(jnp.float32))[:, None]
    lam_re = lam_re.astype(jnp.float32)
    lam_im = lam_im.astype(jnp.float32)
    mag = jnp.exp(lam_re * dt)
    ang = lam_im * dt
    abar_re, abar_im = mag * jnp.cos(ang), mag * jnp.sin(ang)
    den = lam_re * lam_re + lam_im * lam_im
    nr, ni = abar_re - 1.0, abar_im
    f_re = (nr * lam_re + ni * lam_im) / den
    f_im = (ni * lam_re - nr * lam_im) / den
    bb_re = f_re[..., None] * b_re - f_im[..., None] * b_im
    bb_im = f_re[..., None] * b_im + f_im[..., None] * b_re
    a_re = jnp.broadcast_to(abar_re, (bsz, S5_CHUNK, groups, S5_STATE))
    a_im = jnp.broadcast_to(abar_im, (bsz, S5_CHUNK, groups, S5_STATE))
    u_blocks = u.reshape(bsz, n_chunks, S5_CHUNK, groups, gw).transpose(1, 0, 2, 3, 4)

    def combine(e1, e2):
        a1r, a1i, b1r, b1i = e1
        a2r, a2i, b2r, b2i = e2
        return (a2r * a1r - a2i * a1i,
                a2r * a1i + a2i * a1r,
                a2r * b1r - a2i * b1i + b2r,
                a2r * b1i + a2i * b1r + b2i)

    def step(carry, u_c):
        xr0, xi0 = carry
        bu_re = jnp.einsum('btgh,gph->btgp', u_c, bb_re)
        bu_im = jnp.einsum('btgh,gph->btgp', u_c, bb_im)
        ar, ai, sr, si = lax.associative_scan(combine, (a_re, a_im, bu_re, bu_im), axis=1)
        xr = ar * xr0[:, None] - ai * xi0[:, None] + sr
        xi = ar * xi0[:, None] + ai * xr0[:, None] + si
        y = jnp.einsum('btgp,ghp->btgh', xr, c_re) - jnp.einsum('btgp,ghp->btgh', xi, c_im)
        return (xr[:, -1], xi[:, -1]), y

    zeros = jnp.zeros((bsz, groups, S5_STATE), jnp.float32)
    _, y = lax.scan(step, (zeros, zeros), u_blocks)
    return y.transpose(1, 0, 2, 3, 4).reshape(bsz, length, groups, gw)


def hybrid_mixer(n, w_in, w_gk2, b_gk, gla_norm_w, lam_re, lam_im, log_dt, b_re, b_im,
                 c_re, c_im, s5_d, w_glu, b_glu, s5_norm_w, w_out):
    bsz, length, _ = n.shape
    proj = n @ w_in
    q, k, v, g_out, gk_lr, u = jnp.split(proj, IN_SPLITS, axis=-1)
    gk = jax.nn.log_sigmoid((gk_lr @ w_gk2 + b_gk).astype(jnp.float32)) / GLA_GATE_NORMALIZER

    def heads(t, d):
        return t.reshape(bsz, length, GLA_HEADS, d).astype(jnp.float32)

    o = gla_chunked(heads(q, GLA_DK), heads(k, GLA_DK), heads(v, GLA_DV), heads(gk, GLA_DK))
    o = rms_norm(o) * gla_norm_w
    o = o.reshape(bsz, length, GLA_WIDTH) * jax.nn.silu(g_out.astype(jnp.float32))
    uf = u.astype(jnp.float32)
    y = s5_ssm(uf.reshape(bsz, length, S5_GROUPS, S5_GROUP), lam_re, lam_im, log_dt,
               b_re, b_im, c_re, c_im).reshape(bsz, length, S5_WIDTH) + s5_d * uf
    y = jax.nn.gelu(y)
    y = y * jax.nn.sigmoid(y @ w_glu + b_glu)
    y = rms_norm(y) * s5_norm_w
    merged = jnp.concatenate([o.astype(n.dtype), y.astype(n.dtype)], axis=-1)
    return merged @ w_out


def setup_inputs(seed: int = 0) -> dict:
    key = jax.random.key(seed)
    ks = iter(jax.random.split(key, 40))

    def nrm(shape, scale):
        return jax.random.normal(next(ks), shape, jnp.float32) * scale

    L_ = DEPTH
    x = nrm((BATCH, SEQ, D_MODEL), 1.0)
    c = nrm((BATCH, D_MODEL), 1.0)
    w_ada = nrm((L_, D_MODEL, N_MOD * D_MODEL), 0.5 * D_MODEL ** -0.5)
    b_ada = nrm((L_, N_MOD * D_MODEL), 0.02)
    ffn1_w_gate = nrm((L_, D_MODEL, D_FF), D_MODEL ** -0.5)
    ffn1_w_up = nrm((L_, D_MODEL, D_FF), D_MODEL ** -0.5)
    ffn1_w_down = nrm((L_, D_FF, D_MODEL), D_FF ** -0.5)
    w_in = nrm((L_, D_MODEL, IN_WIDTH), D_MODEL ** -0.5)
    w_gk2 = nrm((L_, GLA_GATE_RANK, GLA_KEY_WIDTH), GLA_GATE_RANK ** -0.5)
    b_gk = nrm((L_, GLA_KEY_WIDTH), 0.1)
    gla_norm_w = 1.0 + nrm((L_, GLA_DV), 0.02)
    n_idx = jnp.arange(S5_STATE, dtype=jnp.float32)
    s5_lambda_re = -0.5 + nrm((L_, S5_GROUPS, S5_STATE), 0.01)
    s5_lambda_im = math.pi * n_idx + nrm((L_, S5_GROUPS, S5_STATE), 0.01)
    s5_log_dt = jax.random.uniform(next(ks), (L_, S5_GROUPS), jnp.float32,
                                   math.log(1e-3), math.log(1e-1))
    s5_b_re = nrm((L_, S5_GROUPS, S5_STATE, S5_GROUP), (2.0 * S5_GROUP) ** -0.5)
    s5_b_im = nrm((L_, S5_GROUPS, S5_STATE, S5_GROUP), (2.0 * S5_GROUP) ** -0.5)
    s5_c_re = nrm((L_, S5_GROUPS, S5_GROUP, S5_STATE), (2.0 * S5_STATE) ** -0.5)
    s5_c_im = nrm((L_, S5_GROUPS, S5_GROUP, S5_STATE), (2.0 * S5_STATE) ** -0.5)
    s5_d = nrm((L_, S5_WIDTH), 1.0)
    w_glu = nrm((L_, S5_WIDTH, S5_WIDTH), S5_WIDTH ** -0.5)
    b_glu = nrm((L_, S5_WIDTH), 0.02)
    s5_norm_w = 1.0 + nrm((L_, S5_WIDTH), 0.02)
    w_out = nrm((L_, MIX_WIDTH, D_MODEL), MIX_WIDTH ** -0.5)
    ffn2_w_gate = nrm((L_, D_MODEL, D_FF), D_MODEL ** -0.5)
    ffn2_w_up = nrm((L_, D_MODEL, D_FF), D_MODEL ** -0.5)
    ffn2_w_down = nrm((L_, D_FF, D_MODEL), D_FF ** -0.5)
    final_norm_w = 1.0 + nrm((D_MODEL,), 0.02)
    return {"x": x, "c": c, "w_ada": w_ada, "b_ada": b_ada,
            "ffn1_w_gate": ffn1_w_gate, "ffn1_w_up": ffn1_w_up, "ffn1_w_down": ffn1_w_down,
            "w_in": w_in, "w_gk2": w_gk2, "b_gk": b_gk, "gla_norm_w": gla_norm_w,
            "s5_lambda_re": s5_lambda_re, "s5_lambda_im": s5_lambda_im, "s5_log_dt": s5_log_dt,
            "s5_b_re": s5_b_re, "s5_b_im": s5_b_im, "s5_c_re": s5_c_re, "s5_c_im": s5_c_im,
            "s5_d": s5_d, "w_glu": w_glu, "b_glu": b_glu, "s5_norm_w": s5_norm_w,
            "w_out": w_out,
            "ffn2_w_gate": ffn2_w_gate, "ffn2_w_up": ffn2_w_up, "ffn2_w_down": ffn2_w_down,
            "final_norm_w": final_norm_w}


def reference(x, c, w_ada, b_ada, ffn1_w_gate, ffn1_w_up, ffn1_w_down, w_in, w_gk2, b_gk,
              gla_norm_w, s5_lambda_re, s5_lambda_im, s5_log_dt, s5_b_re, s5_b_im, s5_c_re,
              s5_c_im, s5_d, w_glu, b_glu, s5_norm_w, w_out, ffn2_w_gate, ffn2_w_up,
              ffn2_w_down, final_norm_w):
    h = x
    for l in range(DEPTH):
        mod = jax.nn.silu(c) @ w_ada[l] + b_ada[l]
        sh1, sc1, g1, sh2, sc2, g2, sh3, sc3, g3 = jnp.split(mod, N_MOD, axis=-1)
        n = modulate(h, sh1, sc1)
        h = h + 0.5 * g1[:, None, :] * swiglu(n, ffn1_w_gate[l], ffn1_w_up[l], ffn1_w_down[l])
        n = modulate(h, sh2, sc2)
        mix = hybrid_mixer(n, w_in[l], w_gk2[l], b_gk[l], gla_norm_w[l], s5_lambda_re[l],
                           s5_lambda_im[l], s5_log_dt[l], s5_b_re[l], s5_b_im[l], s5_c_re[l],
                           s5_c_im[l], s5_d[l], w_glu[l], b_glu[l], s5_norm_w[l], w_out[l])
        h = h + g2[:, None, :] * mix
        n = modulate(h, sh3, sc3)
        h = h + 0.5 * g3[:, None, :] * swiglu(n, ffn2_w_gate[l], ffn2_w_up[l], ffn2_w_down[l])
    return rms_norm(h) * final_norm_w
```

```python
import functools
import math

import jax
import jax.numpy as jnp
from jax import lax
from jax.experimental import pallas as pl
from jax.experimental.pallas import tpu as pltpu

F32 = jnp.float32
BF16 = jnp.bfloat16
EPS = 1e-6

N_MOD = 9
GLA_HEADS = 4
GLA_GATE_NORMALIZER = 16.0
GLA_CHUNK = 64
GLA_GATE_RANK = 16
S5_GROUP = 16
S5_STATE = 64
S5_BLOCK = 16
FF_ALIGN = 1024
LANES = 128

V7X_VMEM_BYTES = 64 * 1024 * 1024
VMEM_LIMIT_BYTES = V7X_VMEM_BYTES - 8 * 1024 * 1024

_HI = lax.Precision.HIGHEST
_NT = (((1,), (1,)), ((), ()))


def _params(*sem):
    return pltpu.CompilerParams(dimension_semantics=sem, vmem_limit_bytes=VMEM_LIMIT_BYTES)


def _tile(dim, want, align=LANES):
    if dim <= want:
        return dim
    t = (want // align) * align
    while t > align and dim % t:
        t -= align
    assert dim % t == 0, (dim, want, align)
    return t


def _silu(x):
    return x * jax.nn.sigmoid(x)


def _ada_kernel(c_ref, w_ref, b_ref, o_ref):
    s = _silu(c_ref[...]).astype(BF16)
    o_ref[...] = jnp.dot(s, w_ref[...].astype(BF16), preferred_element_type=F32) + b_ref[...]


def _ada_mod(c, w, b):
    bsz, d = c.shape
    n = w.shape[1]
    rows = 8
    c8 = jnp.zeros((rows, d), F32).at[:bsz].set(c)
    tn = _tile(n, 512)
    out = pl.pallas_call(
        _ada_kernel,
        grid=(n // tn,),
        in_specs=[pl.BlockSpec((rows, d), lambda j: (0, 0)),
                  pl.BlockSpec((d, tn), lambda j: (0, j)),
                  pl.BlockSpec((1, tn), lambda j: (0, j))],
        out_specs=pl.BlockSpec((rows, tn), lambda j: (0, j)),
        out_shape=jax.ShapeDtypeStruct((rows, n), F32),
        compiler_params=_params("parallel"),
        name="ada_mod",
    )(c8, w, b.reshape(1, n))
    return out[:bsz]


def _norm_mod_kernel(x_ref, sh_ref, sc_ref, o_ref):
    x = x_ref[...]
    ms = jnp.mean(x * x, axis=-1, keepdims=True)
    o_ref[...] = (x * lax.rsqrt(ms + EPS) * (1.0 + sc_ref[...]) + sh_ref[...]).astype(o_ref.dtype)


def _norm_mod(x, mod4, i_shift, i_scale, seq):
    t, d = x.shape
    tr = _tile(seq, 256, 8)
    per_b = seq // tr
    return pl.pallas_call(
        _norm_mod_kernel,
        grid=(t // tr,),
        in_specs=[pl.BlockSpec((tr, d), lambda i: (i, 0)),
                  pl.BlockSpec((None, None, 1, d), lambda i: (i // per_b, i_shift, 0, 0)),
                  pl.BlockSpec((None, None, 1, d), lambda i: (i // per_b, i_scale, 0, 0))],
        out_specs=pl.BlockSpec((tr, d), lambda i: (i, 0)),
        out_shape=jax.ShapeDtypeStruct((t, d), BF16),
        compiler_params=_params("parallel"),
        name="norm_mod",
    )(x, mod4, mod4)


def _norm_gain_kernel(x_ref, w_ref, o_ref):
    x = x_ref[...]
    ms = jnp.mean(x * x, axis=-1, keepdims=True)
    o_ref[...] = x * lax.rsqrt(ms + EPS) * w_ref[...]


def _norm_gain(x, w):
    t, d = x.shape
    tr = _tile(t, 256, 8)
    return pl.pallas_call(
        _norm_gain_kernel,
        grid=(t // tr,),
        in_specs=[pl.BlockSpec((tr, d), lambda i: (i, 0)),
                  pl.BlockSpec((1, d), lambda i: (0, 0))],
        out_specs=pl.BlockSpec((tr, d), lambda i: (i, 0)),
        out_shape=jax.ShapeDtypeStruct((t, d), F32),
        compiler_params=_params("parallel"),
        name="norm_gain",
    )(x, w.reshape(1, d))


def _ffn_up_kernel(n_ref, wg_ref, wu_ref, o_ref):
    n = n_ref[...]
    g = jnp.dot(n, wg_ref[...], preferred_element_type=F32)
    u = jnp.dot(n, wu_ref[...], preferred_element_type=F32)
    o_ref[...] = (_silu(g) * u).astype(o_ref.dtype)


def _ffn_up(n, wg, wu):
    t, d = n.shape
    f = wg.shape[1]
    tm = _tile(t, 1024)
    tf = _tile(f, 512)
    return pl.pallas_call(
        _ffn_up_kernel,
        grid=(t // tm, f // tf),
        in_specs=[pl.BlockSpec((tm, d), lambda i, j: (i, 0)),
                  pl.BlockSpec((d, tf), lambda i, j: (0, j)),
                  pl.BlockSpec((d, tf), lambda i, j: (0, j))],
        out_specs=pl.BlockSpec((tm, tf), lambda i, j: (i, j)),
        out_shape=jax.ShapeDtypeStruct((t, f), BF16),
        compiler_params=_params("parallel", "arbitrary"),
        name="ffn_up",
    )(n, wg, wu)


def _ffn_down_kernel(a_ref, w_ref, x_ref, g_ref, o_ref, *, coef, nk):
    k = pl.program_id(2)
    p = jnp.dot(a_ref[...], w_ref[...], preferred_element_type=F32)

    @pl.when(k == 0)
    def _():
        o_ref[...] = p

    @pl.when(k > 0)
    def _():
        o_ref[...] += p

    @pl.when(k == nk - 1)
    def _():
        o_ref[...] = x_ref[...] + (coef * g_ref[...]) * o_ref[...]


def _ffn_down(a, w, x, mod4, i_gate, coef, seq):
    t, f = a.shape
    d = w.shape[1]
    tm = _tile(seq, 1024)
    tn = _tile(d, 2048)
    tk = _tile(f, 1024)
    per_b = seq // tm
    nk = f // tk
    return pl.pallas_call(
        functools.partial(_ffn_down_kernel, coef=coef, nk=nk),
        grid=(t // tm, d // tn, nk),
        in_specs=[pl.BlockSpec((tm, tk), lambda i, j, k: (i, k)),
                  pl.BlockSpec((tk, tn), lambda i, j, k: (k, j)),
                  pl.BlockSpec((tm, tn), lambda i, j, k: (i, j)),
                  pl.BlockSpec((None, None, 1, tn), lambda i, j, k: (i // per_b, i_gate, 0, j))],
        out_specs=pl.BlockSpec((tm, tn), lambda i, j, k: (i, j)),
        out_shape=jax.ShapeDtypeStruct((t, d), F32),
        compiler_params=_params("parallel", "parallel", "arbitrary"),
        name="ffn_down",
    )(a, w, x, mod4)


def _proj_kernel(n_ref, w_ref, o_ref):
    o_ref[...] = jnp.dot(n_ref[...], w_ref[...], preferred_element_type=F32).astype(o_ref.dtype)


def _proj(n, w, want_tn):
    t, d = n.shape
    nn = w.shape[1]
    tm = _tile(t, 1024)
    tn = _tile(nn, want_tn)
    return pl.pallas_call(
        _proj_kernel,
        grid=(t // tm, nn // tn),
        in_specs=[pl.BlockSpec((tm, d), lambda i, j: (i, 0)),
                  pl.BlockSpec((d, tn), lambda i, j: (0, j))],
        out_specs=pl.BlockSpec((tm, tn), lambda i, j: (i, j)),
        out_shape=jax.ShapeDtypeStruct((t, nn), BF16),
        compiler_params=_params("parallel", "arbitrary"),
        name="proj_in",
    )(n, w)


def _gla_kernel(q_ref, k_ref, v_ref, go_ref, lr_ref, w2_ref, bgk_ref, nw_ref, o_ref, st_ref,
                *, chunk, scale):
    @pl.when(pl.program_id(2) == 0)
    def _():
        st_ref[...] = jnp.zeros_like(st_ref)

    lt, dk = q_ref.shape
    z = jnp.dot(lr_ref[...], w2_ref[...], preferred_element_type=F32) + bgk_ref[...]
    g = (jnp.minimum(z, 0.0) - jnp.log(1.0 + jnp.exp(-jnp.abs(z)))) * (1.0 / GLA_GATE_NORMALIZER)
    rowc = lax.broadcasted_iota(jnp.int32, (lt, dk), 0) % chunk
    b = g
    s = 1
    while s < chunk:
        b = b + jnp.where(rowc >= s, pltpu.roll(b, s, axis=0), 0.0)
        s *= 2
    q = q_ref[...].astype(F32)
    k = k_ref[...].astype(F32)
    causal = (lax.broadcasted_iota(jnp.int32, (chunk, chunk), 0)
              >= lax.broadcasted_iota(jnp.int32, (chunk, chunk), 1))
    nw = nw_ref[...]
    for c in range(lt // chunk):
        sl = slice(c * chunk, (c + 1) * chunk)
        bc = b[sl]
        bl = bc[chunk - 1:chunk]
        qt = (q[sl] * jnp.exp(bc) * scale).astype(BF16)
        kt = (k[sl] * jnp.exp(-bc)).astype(BF16)
        kd = (k[sl] * jnp.exp(bl - bc)).astype(BF16)
        vc = v_ref[sl, :]
        att = lax.dot_general(qt, kt, _NT, preferred_element_type=F32)
        att = jnp.where(causal, att, 0.0).astype(BF16)
        st = st_ref[...]
        o = (jnp.dot(att, vc, preferred_element_type=F32)
             + lax.dot_general(qt, st.astype(BF16), _NT, preferred_element_type=F32))
        vt = vc.astype(F32).T.astype(BF16)
        st_ref[...] = st * jnp.exp(bl) + jnp.dot(vt, kd, preferred_element_type=F32)
        ms = jnp.mean(o * o, axis=-1, keepdims=True)
        o = o * lax.rsqrt(ms + EPS) * nw
        o_ref[sl, :] = (o * _silu(go_ref[sl, :].astype(F32))).astype(o_ref.dtype)


def _gla(p, col0, w2p, b_gk, norm_w, bsz, seq, dk, dv):
    t = p.shape[0]
    heads = GLA_HEADS
    lt = _tile(seq, 512, GLA_CHUNK)
    nt = seq // lt
    lr_w = w2p.shape[0]
    k0 = col0 + heads * dk
    v0 = k0 + heads * dk
    g0 = v0 + heads * dv
    lr0 = g0 + heads * dv
    assert col0 % dk == 0 and v0 % dv == 0 and lr0 % lr_w == 0
    row = lambda b, h, i: b * nt + i
    return pl.pallas_call(
        functools.partial(_gla_kernel, chunk=GLA_CHUNK, scale=dk ** -0.5),
        grid=(bsz, heads, nt),
        in_specs=[pl.BlockSpec((lt, dk), lambda b, h, i: (row(b, h, i), col0 // dk + h)),
                  pl.BlockSpec((lt, dk), lambda b, h, i: (row(b, h, i), k0 // dk + h)),
                  pl.BlockSpec((lt, dv), lambda b, h, i: (row(b, h, i), v0 // dv + h)),
                  pl.BlockSpec((lt, dv), lambda b, h, i: (row(b, h, i), g0 // dv + h)),
                  pl.BlockSpec((lt, lr_w), lambda b, h, i: (row(b, h, i), lr0 // lr_w)),
                  pl.BlockSpec((lr_w, dk), lambda b, h, i: (0, h)),
                  pl.BlockSpec((1, dk), lambda b, h, i: (0, h)),
                  pl.BlockSpec((1, dv), lambda b, h, i: (0, 0))],
        out_specs=pl.BlockSpec((lt, dv), lambda b, h, i: (row(b, h, i), h)),
        out_shape=jax.ShapeDtypeStruct((t, heads * dv), BF16),
        scratch_shapes=[pltpu.VMEM((dv, dk), F32)],
        compiler_params=_params("parallel", "parallel", "arbitrary"),
        name="gla",
    )(p, p, p, p, p, w2p, b_gk.reshape(1, -1), norm_w.reshape(1, -1))


def _s5_weights_kernel(lr_ref, li_ref, ldt_ref, btre_ref, btim_ref, c2re_ref, c2im_ref,
                       tzt_ref, we_ref, wct_ref, al_ref, be_ref):
    p = S5_STATE
    h = S5_GROUP
    nb = S5_BLOCK
    w = nb * h
    dt = jnp.exp(ldt_ref[...])
    lr, li = lr_ref[...], li_ref[...]
    mag = jnp.exp(lr * dt)
    ang = li * dt
    ar, ai = mag * jnp.cos(ang), mag * jnp.sin(ang)
    den = lr * lr + li * li
    nr, ni = ar - 1.0, ai
    f_re = (nr * lr + ni * li) / den
    f_im = (ni * lr - nr * li) / den
    bt_re, bt_im = btre_ref[...], btim_ref[...]
    bbt_re = f_re * bt_re - f_im * bt_im
    bbt_im = f_re * bt_im + f_im * bt_re
    pw_re, pw_im = [jnp.ones_like(ar)], [jnp.zeros_like(ar)]
    for _ in range(nb):
        pr, pi = pw_re[-1], pw_im[-1]
        pw_re.append(pr * ar - pi * ai)
        pw_im.append(pr * ai + pi * ar)
    rept = (lax.broadcasted_iota(jnp.int32, (w, h), 0) % h
            == lax.broadcasted_iota(jnp.int32, (w, h), 1)).astype(F32)
    rep = lambda a: jnp.dot(rept, a, precision=_HI, preferred_element_type=F32)
    bb_re, bb_im = rep(bbt_re), rep(bbt_im)
    cc_re, cc_im = rep(c2re_ref[...]), rep(c2im_ref[...])
    rowblk = lax.broadcasted_iota(jnp.int32, (w, 2 * p), 0) // h
    lane = lax.broadcasted_iota(jnp.int32, (w, 2 * p), 1)
    zeros = jnp.zeros((w, 2 * p), F32)
    pe_re, pe_im, pc_re, pc_im = zeros, zeros, zeros, zeros
    for s in range(nb):
        sel = rowblk == s
        pe_re = jnp.where(sel, pw_re[nb - 1 - s], pe_re)
        pe_im = jnp.where(sel, pw_im[nb - 1 - s], pe_im)
        pc_re = jnp.where(sel, pw_re[s + 1], pc_re)
        pc_im = jnp.where(sel, pw_im[s + 1], pc_im)
    we = jnp.where(lane < p, bb_re * pe_re - bb_im * pe_im, bb_re * pe_im + bb_im * pe_re)
    we_ref[...] = we.astype(we_ref.dtype)
    wct = jnp.where(lane < p, cc_re * pc_re - cc_im * pc_im, -(cc_re * pc_im + cc_im * pc_re))
    wct_ref[...] = wct.astype(wct_ref.dtype)
    lane_h = lax.broadcasted_iota(jnp.int32, (h, 2 * p), 1)
    csgn = jnp.where(lane_h < p, c2re_ref[...], -c2im_ref[...])
    krev = lax.dot_general(csgn, we, _NT, precision=_HI, preferred_element_type=F32)
    hblk = lax.broadcasted_iota(jnp.int32, (h, w), 1) // h
    for j in range(nb):
        d = (nb - 1 - j) * h
        rolled = krev if d == 0 else pltpu.roll(krev, w - d, axis=1)
        tzt_ref[j * h:(j + 1) * h, :] = jnp.where(hblk <= j, rolled, 0.0).astype(tzt_ref.dtype)
    lane1 = lax.broadcasted_iota(jnp.int32, (1, 2 * p), 1)
    al_ref[...] = pw_re[nb]
    be_ref[...] = jnp.where(lane1 < p, -pw_im[nb], pw_im[nb])


def _s5_weights(lam_re, lam_im, log_dt, b_re, b_im, c_re, c_im):
    g, p = lam_re.shape
    h = S5_GROUP
    w = S5_BLOCK * h
    dup = lambda a: jnp.concatenate([a, a], axis=-1)
    args = (dup(lam_re).reshape(g, 1, 2 * p), dup(lam_im).reshape(g, 1, 2 * p),
            log_dt.reshape(g, 1, 1), dup(b_re.transpose(0, 2, 1)), dup(b_im.transpose(0, 2, 1)),
            dup(c_re), dup(c_im))
    spec = lambda *shape: pl.BlockSpec((None,) + shape, lambda i: (i,) + (0,) * len(shape))
    return pl.pallas_call(
        _s5_weights_kernel,
        grid=(g,),
        in_specs=[spec(1, 2 * p), spec(1, 2 * p), spec(1, 1),
                  spec(h, 2 * p), spec(h, 2 * p), spec(h, 2 * p), spec(h, 2 * p)],
        out_specs=[spec(w, w), spec(w, 2 * p), spec(w, 2 * p), spec(1, 2 * p), spec(1, 2 * p)],
        out_shape=[jax.ShapeDtypeStruct((g, w, w), BF16),
                   jax.ShapeDtypeStruct((g, w, 2 * p), BF16),
                   jax.ShapeDtypeStruct((g, w, 2 * p), BF16),
                   jax.ShapeDtypeStruct((g, 1, 2 * p), F32),
                   jax.ShapeDtypeStruct((g, 1, 2 * p), F32)],
        compiler_params=_params("parallel"),
        name="s5_weights",
    )(*args)


def _s5_kernel(u_ref, tzt_ref, we_ref, wct_ref, al_ref, be_ref, y_ref,
               zs_ref, ut_ref, e_ref, e2_ref, *, ng, nbatch, nblk):
    n = nbatch * nblk
    nb = S5_BLOCK
    h = S5_GROUP
    p2 = 2 * S5_STATE
    w = nb * h
    pc = nb * nb
    r_i = lax.broadcasted_iota(jnp.int32, (pc, pc), 0)
    c_i = lax.broadcasted_iota(jnp.int32, (pc, pc), 1)
    perm = ((r_i % nb) * nb + r_i // nb == c_i).astype(BF16)

    def regroup_in(c, carry):
        r0 = pl.multiple_of(c * pc, pc)
        b0 = pl.multiple_of(c * nb, nb)
        zc = jnp.dot(perm, u_ref[pl.ds(r0, pc), :], preferred_element_type=F32)
        for s in range(nb):
            zs_ref[s, pl.ds(b0, nb), :] = zc[s * nb:(s + 1) * nb, :].astype(zs_ref.dtype)
        return carry

    lax.fori_loop(0, n // nb, regroup_in, 0)
    for s in range(nb):
        zt = zs_ref[s].astype(F32).T
        for g in range(ng):
            ut_ref[g, s * h:(s + 1) * h, :] = zt[g * h:(g + 1) * h, :].astype(ut_ref.dtype)

    for g in range(ng):
        u = ut_ref[g].astype(F32).T.astype(BF16)
        e = jnp.dot(u, we_ref[g], preferred_element_type=F32)
        e_ref[g] = e
        e2_ref[g] = pltpu.roll(e, S5_STATE, axis=1)
    al = [al_ref[:, g * p2:(g + 1) * p2] for g in range(ng)]
    be = [be_ref[:, g * p2:(g + 1) * p2] for g in range(ng)]

    def scan(i, carry):
        rows = pl.ds(i, nbatch, stride=nblk)
        out = []
        for g in range(ng):
            x, x2 = carry[g]
            ec = e_ref[g, rows, :]
            ec2 = e2_ref[g, rows, :]
            e_ref[g, rows, :] = x
            out.append((al[g] * x + be[g] * x2 + ec, al[g] * x2 - be[g] * x + ec2))
        return tuple(out)

    zero = jnp.zeros((nbatch, p2), F32)
    lax.fori_loop(0, nblk, scan, tuple((zero, zero) for _ in range(ng)))

    for g in range(ng):
        cg = e_ref[g].astype(BF16)
        yt = jnp.dot(tzt_ref[g], ut_ref[g], preferred_element_type=F32)
        yt = yt + lax.dot_general(wct_ref[g], cg, _NT, preferred_element_type=F32)
        ut_ref[g] = yt.astype(ut_ref.dtype)
    for j in range(nb):
        wj = jnp.concatenate([ut_ref[g, j * h:(j + 1) * h, :] for g in range(ng)], axis=0)
        zs_ref[j] = wj.astype(F32).T.astype(zs_ref.dtype)

    def regroup_out(c, carry):
        r0 = pl.multiple_of(c * pc, pc)
        b0 = pl.multiple_of(c * nb, nb)
        zc = jnp.concatenate([zs_ref[j, pl.ds(b0, nb), :] for j in range(nb)], axis=0)
        y_ref[pl.ds(r0, pc), :] = jnp.dot(perm, zc, preferred_element_type=F32).astype(y_ref.dtype)
        return carry

    lax.fori_loop(0, n // nb, regroup_out, 0)


def _s5(p, col0, tzt, we, wct, al, be, nbatch):
    t = p.shape[0]
    groups = tzt.shape[0]
    ng = LANES // S5_GROUP
    assert groups % ng == 0 and col0 % LANES == 0
    cw = ng * S5_GROUP
    p2 = 2 * S5_STATE
    w = S5_BLOCK * S5_GROUP
    n = t // S5_BLOCK
    nblk = n // nbatch
    al2 = al.reshape(groups // ng, 1, ng * p2)
    be2 = be.reshape(groups // ng, 1, ng * p2)
    cb = col0 // cw
    return pl.pallas_call(
        functools.partial(_s5_kernel, ng=ng, nbatch=nbatch, nblk=nblk),
        grid=(groups // ng,),
        in_specs=[pl.BlockSpec((t, cw), lambda i: (0, cb + i)),
                  pl.BlockSpec((ng, w, w), lambda i: (i, 0, 0)),
                  pl.BlockSpec((ng, w, p2), lambda i: (i, 0, 0)),
                  pl.BlockSpec((ng, w, p2), lambda i: (i, 0, 0)),
                  pl.BlockSpec((None, 1, ng * p2), lambda i: (i, 0, 0)),
                  pl.BlockSpec((None, 1, ng * p2), lambda i: (i, 0, 0))],
        out_specs=pl.BlockSpec((t, cw), lambda i: (0, i)),
        out_shape=jax.ShapeDtypeStruct((t, groups * S5_GROUP), BF16),
        scratch_shapes=[pltpu.VMEM((S5_BLOCK, n, cw), BF16),
                        pltpu.VMEM((ng, w, n), BF16),
                        pltpu.VMEM((ng, n, p2), F32),
                        pltpu.VMEM((ng, n, p2), F32)],
        compiler_params=_params("parallel"),
        name="s5",
    )(p, tzt, we, wct, al2, be2)


def _gelu_tanh(x):
    return 0.5 * x * (1.0 + jnp.tanh(math.sqrt(2.0 / math.pi) * (x + 0.044715 * (x * x * x))))


def _mixer_out_kernel(o_ref, y_ref, u_ref, d_ref, wglu_ref, bglu_ref, nw_ref, wout_ref,
                      h_ref, g_ref, out_ref, m_ref):
    @pl.when(pl.program_id(1) == 0)
    def _():
        gw = o_ref.shape[1]
        sw = u_ref.shape[1]
        y = y_ref[...].astype(F32) + d_ref[...] * u_ref[...].astype(F32)
        y = _gelu_tanh(y)
        z = jnp.dot(y.astype(BF16), wglu_ref[...], preferred_element_type=F32) + bglu_ref[...]
        y = y * jax.nn.sigmoid(z)
        ms = jnp.mean(y * y, axis=-1, keepdims=True)
        y = y * lax.rsqrt(ms + EPS) * nw_ref[...]
        m_ref[:, 0:gw] = o_ref[...]
        m_ref[:, gw:gw + sw] = y.astype(m_ref.dtype)

    mix = jnp.dot(m_ref[...], wout_ref[...], preferred_element_type=F32)
    out_ref[...] = h_ref[...] + g_ref[...] * mix


def _mixer_out(o, y, p, u_col0, s5_d, w_glu, b_glu, s5_norm_w, w_out, h, mod4, i_gate, seq):
    t, gw = o.shape
    sw = y.shape[1]
    d = w_out.shape[1]
    tm = _tile(seq, 512, 8)
    per_b = seq // tm
    tn = _tile(d, 512)
    ub = u_col0 // sw
    assert u_col0 % sw == 0
    return pl.pallas_call(
        _mixer_out_kernel,
        grid=(t // tm, d // tn),
        in_specs=[pl.BlockSpec((tm, gw), lambda i, j: (i, 0)),
                  pl.BlockSpec((tm, sw), lambda i, j: (i, 0)),
                  pl.BlockSpec((tm, sw), lambda i, j: (i, ub)),
                  pl.BlockSpec((1, sw), lambda i, j: (0, 0)),
                  pl.BlockSpec((sw, sw), lambda i, j: (0, 0)),
                  pl.BlockSpec((1, sw), lambda i, j: (0, 0)),
                  pl.BlockSpec((1, sw), lambda i, j: (0, 0)),
                  pl.BlockSpec((gw + sw, tn), lambda i, j: (0, j)),
                  pl.BlockSpec((tm, tn), lambda i, j: (i, j)),
                  pl.BlockSpec((None, None, 1, tn), lambda i, j: (i // per_b, i_gate, 0, j))],
        out_specs=pl.BlockSpec((tm, tn), lambda i, j: (i, j)),
        out_shape=jax.ShapeDtypeStruct((t, d), F32),
        scratch_shapes=[pltpu.VMEM((tm, gw + sw), BF16)],
        compiler_params=_params("parallel", "arbitrary"),
        name="mixer_out",
    )(o, y, p, s5_d.reshape(1, sw), w_glu, b_glu.reshape(1, sw), s5_norm_w.reshape(1, sw),
      w_out, h, mod4)


def _pad_cols(w, mult):
    extra = (-w.shape[1]) % mult
    return jnp.pad(w, ((0, 0), (0, extra))) if extra else w


def _swiglu_block(n, x, w_gate, w_up, w_down, mod4, i_gate, seq):
    wg = _pad_cols(w_gate.astype(BF16), FF_ALIGN)
    wu = _pad_cols(w_up.astype(BF16), FF_ALIGN)
    wd = jnp.pad(w_down.astype(BF16), ((0, (-w_down.shape[0]) % FF_ALIGN), (0, 0)))
    act = _ffn_up(n, wg, wu)
    return _ffn_down(act, wd, x, mod4, i_gate, 0.5, seq)


def _mixer_block(n, h, w_in, w_gk2, b_gk, gla_norm_w, lam_re, lam_im, log_dt, b_re, b_im,
                 c_re, c_im, s5_d, w_glu, b_glu, s5_norm_w, w_out, mod4, i_gate, bsz, seq):
    groups = lam_re.shape[0]
    s5_w = groups * S5_GROUP
    key_w = w_gk2.shape[1]
    dk = key_w // GLA_HEADS
    gla_w = (w_in.shape[1] - s5_w - GLA_GATE_RANK - 2 * key_w) // 2
    dv = gla_w // GLA_HEADS
    n_gla = 2 * key_w + 2 * gla_w
    lr_w = LANES
    w_in_b = w_in.astype(BF16)
    w_cat = jnp.concatenate(
        [w_in_b[:, n_gla + GLA_GATE_RANK:], w_in_b[:, :n_gla],
         _pad_cols(w_in_b[:, n_gla:n_gla + GLA_GATE_RANK], 2 * lr_w)], axis=1)
    w2p = jnp.pad(w_gk2.astype(BF16), ((0, lr_w - GLA_GATE_RANK), (0, 0)))

    p = _proj(n, w_cat, 1024)
    o = _gla(p, s5_w, w2p, b_gk, gla_norm_w, bsz, seq, dk, dv)
    tzt, we, wct, al, be = _s5_weights(lam_re, lam_im, log_dt, b_re, b_im, c_re, c_im)
    y = _s5(p, 0, tzt, we, wct, al, be, bsz)
    return _mixer_out(o, y, p, 0, s5_d, w_glu.astype(BF16), b_glu, s5_norm_w,
                      w_out.astype(BF16), h, mod4, i_gate, seq)


def kernel(x, c, w_ada, b_ada, ffn1_w_gate, ffn1_w_up, ffn1_w_down, w_in, w_gk2, b_gk, gla_norm_w, s5_lambda_re, s5_lambda_im, s5_log_dt, s5_b_re, s5_b_im, s5_c_re, s5_c_im, s5_d, w_glu, b_glu, s5_norm_w, w_out, ffn2_w_gate, ffn2_w_up, ffn2_w_down, final_norm_w):
    bsz, seq, d = x.shape
    depth = w_ada.shape[0]
    h = x.reshape(bsz * seq, d)
    for l in range(depth):
        mod4 = _ada_mod(c, w_ada[l], b_ada[l]).reshape(bsz, N_MOD, 1, d)
        n = _norm_mod(h, mod4, 0, 1, seq)
        h = _swiglu_block(n, h, ffn1_w_gate[l], ffn1_w_up[l], ffn1_w_down[l], mod4, 2, seq)
        n = _norm_mod(h, mod4, 3, 4, seq)
        h = _mixer_block(n, h, w_in[l], w_gk2[l], b_gk[l], gla_norm_w[l], s5_lambda_re[l],
                         s5_lambda_im[l], s5_log_dt[l], s5_b_re[l], s5_b_im[l], s5_c_re[l],
                         s5_c_im[l], s5_d[l], w_glu[l], b_glu[l], s5_norm_w[l], w_out[l],
                         mod4, 5, bsz, seq)
        n = _norm_mod(h, mod4, 6, 7, seq)
        h = _swiglu_block(n, h, ffn2_w_gate[l], ffn2_w_up[l], ffn2_w_down[l], mod4, 8, seq)
    return _norm_gain(h, final_norm_w).reshape(bsz, seq, d)
```
